```python
import jax
import jax.numpy as jnp
from jax import lax
import numpy as np

D_MODEL = 2048
BATCH = 8
SEQ = 2048
DEPTH = 4

MEM_LEN = 256
N_EVEN = (DEPTH + 1) // 2
N_ODD = DEPTH // 2
EPS = 1e-6
Q_BLOCK = 128

SB_HEADS = 8
SB_HEAD_DIM = 128
GDN_HEADS = 8
GDN_DK = 128
GDN_DV = 128
GDN_CONV = 4
GDN_CHUNK = 64
MLA_HEADS = 8
MLA_Q_RANK = 512
MLA_KV_RANK = 512
MLA_NOPE = 128
MLA_ROPE = 64
MLA_V = 128
ROPE_THETA = 10000.0
GLA_HEADS = 4
GLA_DK = 128
GLA_DV = 256
GLA_GATE_RANK = 16
GLA_GATE_TAU = 16.0
GLA_CHUNK = 16
XA_HEADS = 4
XA_HEAD_DIM = 128
D_FF = 5632
FFN_CONV = 3

SB_W = SB_HEADS * SB_HEAD_DIM
GDN_KW = GDN_HEADS * GDN_DK
GDN_VW = GDN_HEADS * GDN_DV
GDN_QKV_W = 2 * GDN_KW + GDN_VW
EVEN_IN = 3 * SB_W + GDN_QKV_W + 2 * GDN_HEADS + GDN_VW
EVEN_MIX = SB_W + GDN_VW
MLA_QK = MLA_NOPE + MLA_ROPE
GLA_KW = GLA_HEADS * GLA_DK
GLA_VW = GLA_HEADS * GLA_DV
ODD_IN = MLA_Q_RANK + MLA_KV_RANK + MLA_ROPE + 2 * GLA_KW + GLA_VW + GLA_GATE_RANK + GLA_VW
ODD_MIX = MLA_HEADS * MLA_V + GLA_VW
XA_W = XA_HEADS * XA_HEAD_DIM

kernel_name = 'hybrid_sb_gdn_mla_gla_trunk'


def rmsnorm(x, g):
    xf = x.astype(jnp.float32)
    y = xf * lax.rsqrt(jnp.mean(xf * xf, axis=-1, keepdims=True) + EPS)
    return (y * g.astype(jnp.float32)).astype(x.dtype)


def l2norm(x):
    xf = x.astype(jnp.float32)
    return xf * lax.rsqrt(jnp.sum(xf * xf, axis=-1, keepdims=True) + EPS)


def split_cols(x, sizes):
    return jnp.split(x, [int(i) for i in np.cumsum(sizes)[:-1]], axis=-1)


def split_heads(x, n_heads):
    b, s, _ = x.shape
    return x.reshape(b, s, n_heads, -1).transpose(0, 2, 1, 3)


def merge_heads(x):
    b, h, s, d = x.shape
    return x.transpose(0, 2, 1, 3).reshape(b, s, h * d)


def causal_dwconv(x, w):
    k = w.shape[0]
    return lax.conv_general_dilated(x, w[:, None, :].astype(x.dtype), window_strides=(1,), padding=[(k - 1, 0)], dimension_numbers=('NWC', 'WIO', 'NWC'), feature_group_count=x.shape[-1])


def rope(x, cos, sin):
    half = x.shape[-1] // 2
    xf = x.astype(jnp.float32)
    x1, x2 = xf[..., :half], xf[..., half:]
    return jnp.concatenate([x1 * cos - x2 * sin, x2 * cos + x1 * sin], axis=-1).astype(x.dtype)


def stick_breaking_attention(q, k, v):
    s_len, d = q.shape[2], q.shape[3]
    scale = d ** -0.5
    outs = []
    for blk in range(s_len // Q_BLOCK):
        q0, q1 = blk * Q_BLOCK, (blk + 1) * Q_BLOCK
        z = jnp.einsum('bhqd,bhkd->bhqk', q[:, :, q0:q1], k[:, :, :q1]).astype(jnp.float32) * scale
        strict = jnp.arange(q1)[None, :] < (q0 + jnp.arange(Q_BLOCK))[:, None]
        log_1m_beta = jnp.where(strict, jax.nn.log_sigmoid(-z), 0.0)
        suffix = lax.cumsum(log_1m_beta, axis=3, reverse=True) - log_1m_beta
        w = jnp.where(strict, jnp.exp(jax.nn.log_sigmoid(z) + suffix), 0.0)
        outs.append(jnp.einsum('bhqk,bhkd->bhqd', w.astype(v.dtype), v[:, :, :q1]))
    return jnp.concatenate(outs, axis=2)


def causal_softmax_attention(q, k, v, scale):
    s_len = q.shape[2]
    outs = []
    for blk in range(s_len // Q_BLOCK):
        q0, q1 = blk * Q_BLOCK, (blk + 1) * Q_BLOCK
        s = jnp.einsum('bhqd,bhkd->bhqk', q[:, :, q0:q1], k[:, :, :q1]).astype(jnp.float32) * scale
        mask = jnp.arange(q1)[None, :] <= (q0 + jnp.arange(Q_BLOCK))[:, None]
        p = jax.nn.softmax(jnp.where(mask, s, -jnp.inf), axis=-1)
        outs.append(jnp.einsum('bhqk,bhkd->bhqd', p.astype(v.dtype), v[:, :, :q1]))
    return jnp.concatenate(outs, axis=2)


def gated_delta_rule(q, k, v, g, beta):
    b, h, s_len, dk = q.shape
    dv = v.shape[-1]
    c = GDN_CHUNK
    n = s_len // c
    f32 = jnp.float32
    q = (q.astype(f32) * dk ** -0.5).reshape(b, h, n, c, dk)
    k = k.astype(f32).reshape(b, h, n, c, dk)
    v = v.astype(f32).reshape(b, h, n, c, dv)
    beta = beta.astype(f32).reshape(b, h, n, c)
    gc = jnp.cumsum(g.astype(f32).reshape(b, h, n, c), axis=-1)
    tri_incl = jnp.tril(jnp.ones((c, c), bool))
    tri_strict = jnp.tril(jnp.ones((c, c), bool), -1)
    decay = jnp.exp(jnp.where(tri_incl, gc[..., :, None] - gc[..., None, :], -jnp.inf))
    k_beta = k * beta[..., None]
    neg_n = -jnp.where(tri_strict, jnp.einsum('bhnid,bhnjd->bhnij', k_beta, k) * decay, 0.0)
    t_inv = jnp.eye(c, dtype=f32) + neg_n
    p = neg_n
    for _ in range(c.bit_length() - 2):
        p = p @ p
        t_inv = t_inv + t_inv @ p
    u = t_inv @ (v * beta[..., None])
    w = t_inv @ (k_beta * jnp.exp(gc)[..., None])
    q_g = q * jnp.exp(gc)[..., None]
    attn = jnp.einsum('bhnid,bhnjd->bhnij', q, k) * decay
    k_dec = k * jnp.exp(gc[..., -1:] - gc)[..., None]
    g_last = jnp.exp(gc[..., -1])

    def step(state, xs):
        q_c, k_c, u_c, w_c, a_c, gl = xs
        v_new = u_c - w_c @ state
        o = q_c @ state + a_c @ v_new
        state = state * gl[..., None, None] + jnp.swapaxes(k_c, -1, -2) @ v_new
        return state, o

    xs = tuple(jnp.moveaxis(t, 2, 0) for t in (q_g, k_dec, u, w, attn, g_last))
    _, o = lax.scan(step, jnp.zeros((b, h, dk, dv), f32), xs)
    return jnp.moveaxis(o, 0, 2).reshape(b, h, s_len, dv)


def gla_chunked(q, k, v, log_a):
    b, h, s_len, dk = q.shape
    dv = v.shape[-1]
    c = GLA_CHUNK
    n = s_len // c
    f32 = jnp.float32
    q = (q.astype(f32) * dk ** -0.5).reshape(b, h, n, c, dk)
    k = k.astype(f32).reshape(b, h, n, c, dk)
    v = v.astype(f32).reshape(b, h, n, c, dv)
    cum = jnp.cumsum(log_a.astype(f32).reshape(b, h, n, c, dk), axis=3)
    q_t = q * jnp.exp(cum)
    k_t = k * jnp.exp(-cum)
    tri = jnp.tril(jnp.ones((c, c), bool))
    attn = jnp.where(tri, jnp.einsum('bhnid,bhnjd->bhnij', q_t, k_t), 0.0)
    o_intra = attn @ v
    cum_last = cum[:, :, :, -1:, :]
    k_dec = k * jnp.exp(cum_last - cum)
    a_last = jnp.exp(cum_last[:, :, :, 0, :])

    def step(state, xs):
        q_c, k_c, v_c, a_c = xs
        o = q_c @ state
        state = state * a_c[..., None] + jnp.swapaxes(k_c, -1, -2) @ v_c
        return state, o

    xs = tuple(jnp.moveaxis(t, 2, 0) for t in (q_t, k_dec, v, a_last))
    _, o_inter = lax.scan(step, jnp.zeros((b, h, dk, dv), f32), xs)
    o = jnp.moveaxis(o_inter, 0, 2) + o_intra
    return o.reshape(b, h, s_len, dv)


def even_mixer(h, w_in, sconv_w, a_log, dt_bias, gdn_norm, w_out):
    sb_q, sb_k, sb_v, gdn_qkv, gdn_a, gdn_b, gdn_gate = split_cols(h @ w_in, [SB_W, SB_W, SB_W, GDN_QKV_W, GDN_HEADS, GDN_HEADS, GDN_VW])
    o_a = merge_heads(stick_breaking_attention(split_heads(sb_q, SB_HEADS), split_heads(sb_k, SB_HEADS), split_heads(sb_v, SB_HEADS)))
    gq, gk, gv = split_cols(jax.nn.silu(causal_dwconv(gdn_qkv, sconv_w)), [GDN_KW, GDN_KW, GDN_VW])
    log_decay = -jnp.exp(a_log.astype(jnp.float32)) * jax.nn.softplus(gdn_a.astype(jnp.float32) + dt_bias.astype(jnp.float32))
    beta = jax.nn.sigmoid(gdn_b.astype(jnp.float32))
    o_b = gated_delta_rule(l2norm(split_heads(gq, GDN_HEADS)), l2norm(split_heads(gk, GDN_HEADS)), split_heads(gv, GDN_HEADS), log_decay.transpose(0, 2, 1), beta.transpose(0, 2, 1))
    o_b = rmsnorm(o_b, gdn_norm) * jax.nn.silu(split_heads(gdn_gate, GDN_HEADS).astype(jnp.float32))
    mix = jnp.concatenate([o_a, merge_heads(o_b).astype(h.dtype)], axis=-1)
    return mix @ w_out


def odd_mixer(h, positions, w_in, q_norm, kv_norm, w_uq, w_ukv, gla_w2, gla_b2, gla_norm, w_out):
    b, s, _ = h.shape
    c_q, c_kv, k_rope, lq, lk, lv, lg, lr = split_cols(h @ w_in, [MLA_Q_RANK, MLA_KV_RANK, MLA_ROPE, GLA_KW, GLA_KW, GLA_VW, GLA_GATE_RANK, GLA_VW])
    q = (rmsnorm(c_q, q_norm) @ w_uq).reshape(b, s, MLA_HEADS, MLA_QK)
    kv = (rmsnorm(c_kv, kv_norm) @ w_ukv).reshape(b, s, MLA_HEADS, MLA_NOPE + MLA_V)
    inv_freq = ROPE_THETA ** (-jnp.arange(0, MLA_ROPE, 2, dtype=jnp.float32) / MLA_ROPE)
    ang = positions.astype(jnp.float32)[..., None] * inv_freq
    cos, sin = jnp.cos(ang), jnp.sin(ang)
    q_pe = rope(q[..., MLA_NOPE:], cos[:, :, None], sin[:, :, None])
    k_pe = rope(k_rope, cos, sin)
    qh = jnp.concatenate([q[..., :MLA_NOPE], q_pe], axis=-1).transpose(0, 2, 1, 3)
    kh = jnp.concatenate([kv[..., :MLA_NOPE], jnp.broadcast_to(k_pe[:, :, None, :], (b, s, MLA_HEADS, MLA_ROPE))], axis=-1).transpose(0, 2, 1, 3)
    vh = kv[..., MLA_NOPE:].transpose(0, 2, 1, 3)
    o_c = merge_heads(causal_softmax_attention(qh, kh, vh, MLA_QK ** -0.5))
    log_a = jax.nn.log_sigmoid((lg @ gla_w2 + gla_b2).astype(jnp.float32)) / GLA_GATE_TAU
    o_d = gla_chunked(split_heads(lq, GLA_HEADS), split_heads(lk, GLA_HEADS), split_heads(lv, GLA_HEADS), split_heads(log_a, GLA_HEADS))
    o_d = rmsnorm(o_d, gla_norm) * jax.nn.silu(split_heads(lr, GLA_HEADS).astype(jnp.float32))
    mix = jnp.concatenate([o_c, merge_heads(o_d).astype(h.dtype)], axis=-1)
    return mix @ w_out


def memory_cross_attention(h, mem_n, wq, wk, wv, wo):
    q = split_heads(h @ wq, XA_HEADS)
    k = split_heads(mem_n @ wk, XA_HEADS)
    v = split_heads(mem_n @ wv, XA_HEADS)
    s = jnp.einsum('bhqd,bhkd->bhqk', q, k).astype(jnp.float32) * XA_HEAD_DIM ** -0.5
    p = jax.nn.softmax(s, axis=-1)
    return merge_heads(jnp.einsum('bhqk,bhkd->bhqd', p.astype(v.dtype), v)) @ wo


def conv_ffn(h, w_in, conv_w, conv_b, w_out):
    uz = causal_dwconv(h @ w_in, conv_w) + conv_b
    u, z = jnp.split(uz, 2, axis=-1)
    return (jax.nn.silu(z) * u) @ w_out


def setup_inputs(seed: int = 0) -> dict:
    key = jax.random.key(seed)
    ks = iter(jax.random.split(key, 64))
    res_scale = (3 * DEPTH) ** -0.5

    def nrm(shape, scale):
        return jax.random.normal(next(ks), shape, jnp.float32) * scale

    def gain(shape):
        return 1.0 + nrm(shape, 0.02)

    dt = jnp.exp(jax.random.uniform(next(ks), (N_EVEN, GDN_HEADS), jnp.float32, minval=float(np.log(1e-3)), maxval=float(np.log(1e-1))))
    return {
        'x': nrm((BATCH, SEQ, D_MODEL), 1.0),
        'mem': nrm((BATCH, MEM_LEN, D_MODEL), 1.0),
        'positions': jnp.arange(SEQ, dtype=jnp.int32)[None, :] + jax.random.randint(next(ks), (BATCH, 1), 0, 4096, dtype=jnp.int32),
        'norm_mix': gain((DEPTH, D_MODEL)),
        'norm_xattn': gain((DEPTH, D_MODEL)),
        'norm_ffn': gain((DEPTH, D_MODEL)),
        'mem_norm': gain((D_MODEL,)),
        'final_norm': gain((D_MODEL,)),
        'ev_w_in': nrm((N_EVEN, D_MODEL, EVEN_IN), D_MODEL ** -0.5),
        'ev_sconv': nrm((N_EVEN, GDN_CONV, GDN_QKV_W), GDN_CONV ** -0.5),
        'ev_a_log': jnp.log(jax.random.uniform(next(ks), (N_EVEN, GDN_HEADS), jnp.float32, minval=1.0, maxval=16.0)),
        'ev_dt_bias': dt + jnp.log(-jnp.expm1(-dt)),
        'ev_gdn_norm': gain((N_EVEN, GDN_DV)),
        'ev_w_out': nrm((N_EVEN, EVEN_MIX, D_MODEL), EVEN_MIX ** -0.5 * res_scale),
        'od_w_in': nrm((N_ODD, D_MODEL, ODD_IN), D_MODEL ** -0.5),
        'od_q_norm': gain((N_ODD, MLA_Q_RANK)),
        'od_kv_norm': gain((N_ODD, MLA_KV_RANK)),
        'od_w_uq': nrm((N_ODD, MLA_Q_RANK, MLA_HEADS * MLA_QK), MLA_Q_RANK ** -0.5),
        'od_w_ukv': nrm((N_ODD, MLA_KV_RANK, MLA_HEADS * (MLA_NOPE + MLA_V)), MLA_KV_RANK ** -0.5),
        'od_gla_w2': nrm((N_ODD, GLA_GATE_RANK, GLA_KW), GLA_GATE_RANK ** -0.5),
        'od_gla_b2': nrm((N_ODD, GLA_KW), 0.01),
        'od_gla_norm': gain((N_ODD, GLA_DV)),
        'od_w_out': nrm((N_ODD, ODD_MIX, D_MODEL), ODD_MIX ** -0.5 * res_scale),
        'xa_wq': nrm((DEPTH, D_MODEL, XA_W), D_MODEL ** -0.5),
        'xa_wk': nrm((DEPTH, D_MODEL, XA_W), D_MODEL ** -0.5),
        'xa_wv': nrm((DEPTH, D_MODEL, XA_W), D_MODEL ** -0.5),
        'xa_wo': nrm((DEPTH, XA_W, D_MODEL), XA_W ** -0.5 * res_scale),
        'ffn_w_in': nrm((DEPTH, D_MODEL, 2 * D_FF), D_MODEL ** -0.5),
        'ffn_conv': nrm((DEPTH, FFN_CONV, 2 * D_FF), FFN_CONV ** -0.5),
        'ffn_conv_b': nrm((DEPTH, 2 * D_FF), 0.01),
        'ffn_w_out': nrm((DEPTH, D_FF, D_MODEL), D_FF ** -0.5 * res_scale),
    }


def reference(x, mem, positions, norm_mix, norm_xattn, norm_ffn, mem_norm, final_norm, ev_w_in, ev_sconv, ev_a_log, ev_dt_bias, ev_gdn_norm, ev_w_out, od_w_in, od_q_norm, od_kv_norm, od_w_uq, od_w_ukv, od_gla_w2, od_gla_b2, od_gla_norm, od_w_out, xa_wq, xa_wk, xa_wv, xa_wo, ffn_w_in, ffn_conv, ffn_conv_b, ffn_w_out):
    mem_n = rmsnorm(mem, mem_norm)
    h = x
    for layer in range(DEPTH):
        hn = rmsnorm(h, norm_mix[layer])
        if layer % 2 == 0:
            e = layer // 2
            h = h + even_mixer(hn, ev_w_in[e], ev_sconv[e], ev_a_log[e], ev_dt_bias[e], ev_gdn_norm[e], ev_w_out[e])
        else:
            o = layer // 2
            h = h + odd_mixer(hn, positions, od_w_in[o], od_q_norm[o], od_kv_norm[o], od_w_uq[o], od_w_ukv[o], od_gla_w2[o], od_gla_b2[o], od_gla_norm[o], od_w_out[o])
        h = h + memory_cross_attention(rmsnorm(h, norm_xattn[layer]), mem_n, xa_wq[layer], xa_wk[layer], xa_wv[layer], xa_wo[layer])
        h = h + conv_ffn(rmsnorm(h, norm_ffn[layer]), ffn_w_in[layer], ffn_conv[layer], ffn_conv_b[layer], ffn_w_out[layer])
    return rmsnorm(h, final_norm)
```

```python
import functools

import jax
import jax.numpy as jnp
from jax import lax
from jax.experimental import pallas as pl
from jax.experimental.pallas import tpu as pltpu

F32 = jnp.float32
BF16 = jnp.bfloat16

EPS = 1e-6
LANE = 128

SB_HEADS = 8
SB_HEAD_DIM = 128
GDN_HEADS = 8
GDN_D = 128
GDN_CONV = 4
GDN_CHUNK = 64
MLA_HEADS = 8
MLA_RANK = 512
MLA_NOPE = 128
MLA_ROPE = 64
MLA_V = 128
ROPE_THETA = 10000.0
GLA_HEADS = 4
GLA_DK = 128
GLA_DV = 256
GLA_GATE_RANK = 16
GLA_GATE_TAU = 16.0
GLA_CHUNK = 16
XA_HEADS = 4
XA_HEAD_DIM = 128
FFN_CONV = 3

VMEM_LIMIT = 56 * 1024 * 1024


def _params(sem, vmem=VMEM_LIMIT):
    return pltpu.CompilerParams(dimension_semantics=sem, vmem_limit_bytes=vmem)


def _dot(a, b):
    return jnp.dot(a, b, preferred_element_type=F32)


def _dot_nt(a, b):
    return lax.dot_general(a, b, (((1,), (1,)), ((), ())), preferred_element_type=F32)


def _dot_tn(a, b):
    return lax.dot_general(a, b, (((0,), (0,)), ((), ())), preferred_element_type=F32)


def _split3(x):
    hi = x.astype(BF16)
    r = x - hi.astype(F32)
    mid = r.astype(BF16)
    lo = (r - mid.astype(F32)).astype(BF16)
    return hi, mid, lo


def _dot_exact_rhs(mask_bf16, x):
    hi, mid, lo = _split3(x)
    return _dot(mask_bf16, hi) + _dot(mask_bf16, mid) + _dot(mask_bf16, lo)


def _softplus(x):
    return jnp.maximum(x, 0.0) + jnp.log1p(jnp.exp(-jnp.abs(x)))


def _sigmoid(x):
    return 1.0 / (1.0 + jnp.exp(-x))


def _rms(x, g):
    ms = jnp.mean(x * x, axis=-1, keepdims=True)
    return x * lax.rsqrt(ms + EPS) * g


def _norm_matmul_kernel(x_ref, g_ref, w_ref, o_ref, xn_ref):
    @pl.when(pl.program_id(1) == 0)
    def _():
        xn_ref[...] = _rms(x_ref[...].astype(F32), g_ref[...]).astype(BF16)

    o_ref[...] = _dot(xn_ref[...], w_ref[...]).astype(o_ref.dtype)


def norm_matmul(x, g, w, out_dtype, *, x_col_block=0, tm=512, tn=512):
    m = x.shape[0]
    k, n = w.shape
    tm = min(tm, m)
    tn = min(tn, n)
    assert m % tm == 0 and n % tn == 0 and k % LANE == 0
    return pl.pallas_call(
        _norm_matmul_kernel,
        grid=(m // tm, n // tn),
        in_specs=[
            pl.BlockSpec((tm, k), lambda i, j: (i, x_col_block)),
            pl.BlockSpec((1, k), lambda i, j: (0, 0)),
            pl.BlockSpec((k, tn), lambda i, j: (0, j)),
        ],
        out_specs=pl.BlockSpec((tm, tn), lambda i, j: (i, j)),
        out_shape=jax.ShapeDtypeStruct((m, n), out_dtype),
        scratch_shapes=[pltpu.VMEM((tm, k), BF16)],
        compiler_params=_params(("parallel", "arbitrary")),
        name="norm_matmul",
    )(x, g.reshape(1, k).astype(F32), w)


def _out_proj_kernel(h_ref, a1_ref, a2_ref, w1_ref, w2_ref, o_ref):
    o_ref[...] = h_ref[...] + _dot(a1_ref[...], w1_ref[...]) + _dot(a2_ref[...], w2_ref[...])


def out_proj(h, a1, a2, w1, w2, *, tm=512, tn=1024):
    m, n = h.shape
    k1, k2 = a1.shape[1], a2.shape[1]
    tm = min(tm, m)
    tn = min(tn, n)
    assert m % tm == 0 and n % tn == 0
    return pl.pallas_call(
        _out_proj_kernel,
        grid=(m // tm, n // tn),
        in_specs=[
            pl.BlockSpec((tm, tn), lambda i, j: (i, j)),
            pl.BlockSpec((tm, k1), lambda i, j: (i, 0)),
            pl.BlockSpec((tm, k2), lambda i, j: (i, 0)),
            pl.BlockSpec((k1, tn), lambda i, j: (0, j)),
            pl.BlockSpec((k2, tn), lambda i, j: (0, j)),
        ],
        out_specs=pl.BlockSpec((tm, tn), lambda i, j: (i, j)),
        out_shape=jax.ShapeDtypeStruct((m, n), F32),
        compiler_params=_params(("parallel", "arbitrary")),
        name="out_proj",
    )(h, a1, a2, w1, w2)


def _rmsnorm_kernel(x_ref, g_ref, o_ref):
    o_ref[...] = _rms(x_ref[...], g_ref[...])


def rmsnorm_rows(x, g, *, tm=512):
    m, d = x.shape
    tm = min(tm, m)
    return pl.pallas_call(
        _rmsnorm_kernel,
        grid=(m // tm,),
        in_specs=[pl.BlockSpec((tm, d), lambda i: (i, 0)), pl.BlockSpec((1, d), lambda i: (0, 0))],
        out_specs=pl.BlockSpec((tm, d), lambda i: (i, 0)),
        out_shape=jax.ShapeDtypeStruct((m, d), F32),
        compiler_params=_params(("parallel",)),
        name="final_rmsnorm",
    )(x, g.reshape(1, d))


def _ffn_kernel(x_ref, g_ref, wu_ref, wz_ref, cu_ref, cz_ref, bu_ref, bz_ref, wo_ref, o_ref,
                hn_ref, acc_ref, pu_ref, pz_ref, *, tiles_per_seq):
    i = pl.program_id(0)
    j = pl.program_id(1)
    tm = x_ref.shape[0]

    @pl.when(j == 0)
    def _():
        hn_ref[...] = _rms(x_ref[...], g_ref[...]).astype(BF16)
        acc_ref[...] = jnp.zeros_like(acc_ref)

    @pl.when(i % tiles_per_seq == 0)
    def _():
        pu_ref[j] = jnp.zeros(pu_ref.shape[1:], F32)
        pz_ref[j] = jnp.zeros(pz_ref.shape[1:], F32)

    hn = hn_ref[...]
    rows = lax.broadcasted_iota(jnp.int32, (8, 1), 0)

    def conv(y, prev_ref, c_ref, b_ref):
        prev = prev_ref[j]
        out = c_ref[2:3, :] * y + b_ref[...]
        for s in (1, 2):
            ys = pltpu.roll(y, s, axis=0)
            top = jnp.where(rows < s, pltpu.roll(prev, s, axis=0), ys[0:8])
            ys = jnp.concatenate([top, ys[8:]], axis=0)
            out = out + c_ref[2 - s:3 - s, :] * ys
        prev_ref[j] = y[tm - 8:tm]
        return out

    u = conv(_dot(hn, wu_ref[...]), pu_ref, cu_ref, bu_ref)
    z = conv(_dot(hn, wz_ref[...]), pz_ref, cz_ref, bz_ref)
    a = (z * _sigmoid(z) * u).astype(BF16)
    acc_ref[...] += _dot(a, wo_ref[...])

    @pl.when(j == pl.num_programs(1) - 1)
    def _():
        o_ref[...] = x_ref[...] + acc_ref[...]


def conv_ffn(h, g, w_in, conv_w, conv_b, w_out, *, seq, tm=512, tf=512):
    m, d = h.shape
    f = w_out.shape[0]
    tm = min(tm, seq)
    assert seq % tm == 0 and f % tf == 0 and tm % 8 == 0
    nf = f // tf
    cw = conv_w.astype(F32)
    cb = conv_b.reshape(1, 2 * f).astype(F32)
    return pl.pallas_call(
        functools.partial(_ffn_kernel, tiles_per_seq=seq // tm),
        grid=(m // tm, nf),
        in_specs=[
            pl.BlockSpec((tm, d), lambda i, j: (i, 0)),
            pl.BlockSpec((1, d), lambda i, j: (0, 0)),
            pl.BlockSpec((d, tf), lambda i, j: (0, j)),
            pl.BlockSpec((d, tf), lambda i, j: (0, j + nf)),
            pl.BlockSpec((FFN_CONV, tf), lambda i, j: (0, j)),
            pl.BlockSpec((FFN_CONV, tf), lambda i, j: (0, j + nf)),
            pl.BlockSpec((1, tf), lambda i, j: (0, j)),
            pl.BlockSpec((1, tf), lambda i, j: (0, j + nf)),
            pl.BlockSpec((tf, d), lambda i, j: (j, 0)),
        ],
        out_specs=pl.BlockSpec((tm, d), lambda i, j: (i, 0)),
        out_shape=jax.ShapeDtypeStruct((m, d), F32),
        scratch_shapes=[
            pltpu.VMEM((tm, d), BF16),
            pltpu.VMEM((tm, d), F32),
            pltpu.VMEM((nf, 8, tf), F32),
            pltpu.VMEM((nf, 8, tf), F32),
        ],
        compiler_params=_params(("arbitrary", "arbitrary")),
        name="conv_ffn",
    )(h, g.reshape(1, d), w_in, w_in, cw, cw, cb, cb, w_out)


def _xattn_kernel(x_ref, g_ref, wq_ref, k_ref, v_ref, wo_ref, o_ref):
    x = x_ref[...]
    q = _dot(_rms(x, g_ref[...]).astype(BF16), wq_ref[...]).astype(BF16)
    scale = XA_HEAD_DIM ** -0.5
    outs = []
    for hd in range(XA_HEADS):
        sl = slice(hd * XA_HEAD_DIM, (hd + 1) * XA_HEAD_DIM)
        s = _dot_nt(q[:, sl], k_ref[:, sl]) * scale
        s = s - jnp.max(s, axis=-1, keepdims=True)
        p = jnp.exp(s)
        p = p / jnp.sum(p, axis=-1, keepdims=True)
        outs.append(_dot(p.astype(BF16), v_ref[:, sl]).astype(BF16))
    o_ref[...] = x + _dot(jnp.concatenate(outs, axis=1), wo_ref[...])


def cross_attention(h, g, wq, k, v, wo, *, seq, mem_len, tm=512):
    m, d = h.shape
    w = wq.shape[1]
    tm = min(tm, seq)
    nq = seq // tm
    return pl.pallas_call(
        _xattn_kernel,
        grid=(m // tm,),
        in_specs=[
            pl.BlockSpec((tm, d), lambda i: (i, 0)),
            pl.BlockSpec((1, d), lambda i: (0, 0)),
            pl.BlockSpec((d, w), lambda i: (0, 0)),
            pl.BlockSpec((mem_len, w), lambda i: (i // nq, 0)),
            pl.BlockSpec((mem_len, w), lambda i: (i // nq, 1)),
            pl.BlockSpec((w, d), lambda i: (0, 0)),
        ],
        out_specs=pl.BlockSpec((tm, d), lambda i: (i, 0)),
        out_shape=jax.ShapeDtypeStruct((m, d), F32),
        compiler_params=_params(("parallel",)),
        name="cross_attention",
    )(h, g.reshape(1, d), wq, k, v, wo)


def _sb_kernel(q_ref, k_ref, v_ref, o_ref):
    qi = pl.program_id(2)
    tq = q_ref.shape[0]
    scale = SB_HEAD_DIM ** -0.5
    q = q_ref[...]
    row = lax.broadcasted_iota(jnp.int32, (tq, tq), 0)
    col = lax.broadcasted_iota(jnp.int32, (tq, tq), 1)
    strict = col < row
    later = jnp.where(row > col, 1.0, 0.0).astype(BF16)

    def block(kb, masked, acc, run):
        start = pl.multiple_of(kb * tq, tq)
        k = k_ref[pl.ds(start, tq), :]
        v = v_ref[pl.ds(start, tq), :]
        z = _dot_nt(q, k) * scale
        sp = _softplus(z)
        lb = -sp
        if masked:
            lb = jnp.where(strict, lb, 0.0)
        hi = lb.astype(BF16)
        lo = (lb - hi.astype(F32)).astype(BF16)
        suffix = _dot(hi, later) + _dot(lo, later) + run
        w = jnp.exp(z - sp + suffix)
        if masked:
            w = jnp.where(strict, w, 0.0)
        acc = acc + _dot(w.astype(BF16), v)
        run = run + jnp.sum(lb, axis=1, keepdims=True)
        return acc, run

    acc, run = block(qi, True, jnp.zeros((tq, SB_HEAD_DIM), F32), jnp.zeros((tq, 1), F32))
    acc, run = lax.fori_loop(0, qi, lambda t, c: block(qi - 1 - t, False, *c), (acc, run))
    o_ref[...] = acc.astype(o_ref.dtype)


def sb_attention(qkv, *, batch, seq, tq=128):
    nq = seq // tq
    hds = SB_HEADS
    return pl.pallas_call(
        _sb_kernel,
        grid=(batch, hds, nq),
        in_specs=[
            pl.BlockSpec((tq, SB_HEAD_DIM), lambda b, h, i: (b * nq + i, h)),
            pl.BlockSpec((seq, SB_HEAD_DIM), lambda b, h, i: (b, hds + h)),
            pl.BlockSpec((seq, SB_HEAD_DIM), lambda b, h, i: (b, 2 * hds + h)),
        ],
        out_specs=pl.BlockSpec((tq, SB_HEAD_DIM), lambda b, h, i: (b * nq + i, h)),
        out_shape=jax.ShapeDtypeStruct((batch * seq, hds * SB_HEAD_DIM), BF16),
        compiler_params=_params(("parallel", "parallel", "arbitrary")),
        name="sb_attention",
    )(qkv, qkv, qkv)


def _gdn_kernel(alog_ref, dtb_ref, q_ref, k_ref, v_ref, gate_ref, ab_ref, cq_ref, ck_ref, cv_ref, gn_ref,
                o_ref, state_ref, hq_ref, hk_ref, hv_ref):
    hd = pl.program_id(1)
    t = pl.program_id(2)
    tt = q_ref.shape[0]
    c = GDN_CHUNK
    p2 = 2 * c

    @pl.when(t == 0)
    def _():
        state_ref[...] = jnp.zeros_like(state_ref)
        hq_ref[...] = jnp.zeros_like(hq_ref)
        hk_ref[...] = jnp.zeros_like(hk_ref)
        hv_ref[...] = jnp.zeros_like(hv_ref)

    rows8 = lax.broadcasted_iota(jnp.int32, (8, 1), 0)

    def conv_silu(x_ref, halo_ref, w_ref):
        x = x_ref[...]
        halo = halo_ref[...]
        y = w_ref[GDN_CONV - 1:GDN_CONV, :] * x
        for s in range(1, GDN_CONV):
            xs = pltpu.roll(x, s, axis=0)
            top = jnp.where(rows8 < s, pltpu.roll(halo, s, axis=0), xs[0:8])
            xs = jnp.concatenate([top, xs[8:]], axis=0)
            y = y + w_ref[GDN_CONV - 1 - s:GDN_CONV - s, :] * xs
        halo_ref[...] = x[tt - 8:tt]
        return y * _sigmoid(y)

    def l2n(x):
        return x * lax.rsqrt(jnp.sum(x * x, axis=-1, keepdims=True) + EPS)

    qn = l2n(conv_silu(q_ref, hq_ref, cq_ref))
    kn = l2n(conv_silu(k_ref, hk_ref, ck_ref))
    vv = conv_silu(v_ref, hv_ref, cv_ref)

    lane = lax.broadcasted_iota(jnp.int32, (tt, LANE), 1)
    ab = ab_ref[...]
    a_col = jnp.sum(jnp.where(lane == hd, ab, 0.0), axis=1, keepdims=True)
    b_col = jnp.sum(jnp.where(lane == hd + GDN_HEADS, ab, 0.0), axis=1, keepdims=True)
    a_scale = jnp.exp(jnp.zeros((1, 1), F32) + alog_ref[hd])
    g_col = -a_scale * _softplus(a_col + dtb_ref[hd])
    beta = _sigmoid(b_col)
    g_b = jnp.broadcast_to(g_col, (tt, LANE))

    ri = lax.broadcasted_iota(jnp.int32, (p2, p2), 0)
    ci = lax.broadcasted_iota(jnp.int32, (p2, p2), 1)
    same = (ri // c) == (ci // c)
    incl = same & (ci <= ri)
    strict = same & (ci < ri)
    lower = jnp.where(incl, 1.0, 0.0).astype(BF16)
    eye = jnp.where(ri == ci, 1.0, 0.0)
    gn = gn_ref[...]
    scale = GDN_D ** -0.5

    state = state_ref[...]
    for pr in range(tt // p2):
        sl = slice(pr * p2, (pr + 1) * p2)
        qc, kc, vc = qn[sl], kn[sl], vv[sl]
        bc = beta[sl]
        gb = g_b[sl]
        gc = _dot_exact_rhs(lower, gb)
        dmat = _dot_exact_rhs(lower, jnp.where(strict, gb, 0.0))
        decay = jnp.where(incl, jnp.exp(dmat), 0.0)
        egc = jnp.exp(gc)
        kb = kc * bc
        kc16 = kc.astype(BF16)
        neg_n = jnp.where(strict, -(_dot_nt(kb.astype(BF16), kc16) * decay), 0.0)
        t_inv = eye + neg_n
        pw = neg_n
        for _ in range(c.bit_length() - 2):
            pw16 = pw.astype(BF16)
            pw = _dot(pw16, pw16)
            t_inv = t_inv + _dot(t_inv.astype(BF16), pw.astype(BF16))
        t16 = t_inv.astype(BF16)
        u = _dot(t16, (vc * bc).astype(BF16))
        w = _dot(t16, (kb * egc).astype(BF16))
        qs = qc * scale
        attn = _dot_nt(qs.astype(BF16), kc16) * decay
        q_g = (qs * egc).astype(BF16)
        vnew = []
        inter = []
        for hf in range(2):
            h_sl = slice(hf * c, (hf + 1) * c)
            g_last = gc[hf * c + c - 1:hf * c + c, :]
            s16 = state.astype(BF16)
            vn = u[h_sl] - _dot(w[h_sl].astype(BF16), s16)
            inter.append(_dot(q_g[h_sl], s16))
            k_dec = kc[h_sl] * jnp.exp(g_last - gc[h_sl])
            state = state * jnp.exp(g_last) + _dot_tn(k_dec.astype(BF16), vn.astype(BF16))
            vnew.append(vn)
        o = jnp.concatenate(inter, axis=0) + _dot(attn.astype(BF16), jnp.concatenate(vnew, axis=0).astype(BF16))
        gt = gate_ref[sl, :]
        o_ref[sl, :] = (_rms(o, gn) * (gt * _sigmoid(gt))).astype(o_ref.dtype)
    state_ref[...] = state


def gdn_mixer(g, ab_cols, sconv, a_log, dt_bias, gdn_norm, *, batch, seq, tt=256):
    tt = min(tt, seq)
    nt = seq // tt
    hds = GDN_HEADS
    d = GDN_D
    tok = lambda off: pl.BlockSpec((tt, d), lambda b, h, t: (b * nt + t, off + h))
    cw = lambda off: pl.BlockSpec((GDN_CONV, d), lambda b, h, t: (0, off + h))
    smem = pl.BlockSpec(memory_space=pltpu.SMEM)
    return pl.pallas_call(
        _gdn_kernel,
        grid=(batch, hds, nt),
        in_specs=[
            smem, smem,
            tok(0), tok(hds), tok(2 * hds), tok(3 * hds),
            pl.BlockSpec((tt, LANE), lambda b, h, t: (b * nt + t, ab_cols)),
            cw(0), cw(hds), cw(2 * hds),
            pl.BlockSpec((1, d), lambda b, h, t: (0, 0)),
        ],
        out_specs=pl.BlockSpec((tt, d), lambda b, h, t: (b * nt + t, h)),
        out_shape=jax.ShapeDtypeStruct((batch * seq, hds * d), BF16),
        scratch_shapes=[
            pltpu.VMEM((d, d), F32),
            pltpu.VMEM((8, d), F32),
            pltpu.VMEM((8, d), F32),
            pltpu.VMEM((8, d), F32),
        ],
        compiler_params=_params(("parallel", "parallel", "arbitrary")),
        name="gdn",
    )(a_log.astype(F32), dt_bias.astype(F32), g, g, g, g, g, sconv, sconv, sconv, gdn_norm.reshape(1, d))


def _rotate_pairs(x, tab):
    prod = x * tab
    lane = lax.broadcasted_iota(jnp.int32, prod.shape, 1)
    return jnp.where(lane < MLA_ROPE, prod + pltpu.roll(prod, MLA_ROPE, axis=1), 0.0)


def _rope_kernel(pos_ref, freq_ref, sign_ref, kr_ref, tab_ref, krot_ref):
    ang = pos_ref[...].astype(F32) * freq_ref[...]
    lane = lax.broadcasted_iota(jnp.int32, ang.shape, 1)
    tab = jnp.where(lane < MLA_ROPE, jnp.cos(ang), jnp.sin(ang) * sign_ref[...])
    tab_ref[...] = tab
    krot_ref[...] = _rotate_pairs(kr_ref[...], tab).astype(krot_ref.dtype)


def rope_prep(pos, proj, kr_col_block, *, tm=512):
    m = pos.shape[0]
    tm = min(tm, m)
    half = MLA_ROPE // 2
    inv = ROPE_THETA ** (-jnp.arange(0, MLA_ROPE, 2, dtype=F32) / MLA_ROPE)
    freq = jnp.concatenate([inv, inv, inv, inv]).reshape(1, LANE)
    ones = jnp.ones((half,), F32)
    sign = jnp.concatenate([ones, ones, -ones, ones]).reshape(1, LANE)
    return pl.pallas_call(
        _rope_kernel,
        grid=(m // tm,),
        in_specs=[
            pl.BlockSpec((tm, 1), lambda i: (i, 0)),
            pl.BlockSpec((1, LANE), lambda i: (0, 0)),
            pl.BlockSpec((1, LANE), lambda i: (0, 0)),
            pl.BlockSpec((tm, LANE), lambda i: (i, kr_col_block)),
        ],
        out_specs=[pl.BlockSpec((tm, LANE), lambda i: (i, 0)), pl.BlockSpec((tm, LANE), lambda i: (i, 0))],
        out_shape=[jax.ShapeDtypeStruct((m, LANE), F32), jax.ShapeDtypeStruct((m, LANE), BF16)],
        compiler_params=_params(("parallel",)),
        name="rope_prep",
    )(pos, freq, sign, proj)


def _mla_kernel(q_ref, tab_ref, kn_ref, kr_ref, v_ref, o_ref):
    qi = pl.program_id(2)
    tq = q_ref.shape[0]
    scale = (MLA_NOPE + MLA_ROPE) ** -0.5
    q_nope = q_ref[:, :MLA_NOPE].astype(BF16)
    q_rot = _rotate_pairs(q_ref[:, MLA_NOPE:], tab_ref[...]).astype(BF16)
    row = lax.broadcasted_iota(jnp.int32, (tq, tq), 0)
    col = lax.broadcasted_iota(jnp.int32, (tq, tq), 1)
    causal = col <= row

    def block(kb, masked, m_run, l_run, acc):
        start = pl.multiple_of(kb * tq, tq)
        s = (_dot_nt(q_nope, kn_ref[pl.ds(start, tq), :]) + _dot_nt(q_rot, kr_ref[pl.ds(start, tq), :])) * scale
        if masked:
            s = jnp.where(causal, s, -jnp.inf)
        m_new = jnp.maximum(m_run, jnp.max(s, axis=-1, keepdims=True))
        alpha = jnp.exp(m_run - m_new)
        p = jnp.exp(s - m_new)
        l_new = alpha * l_run + jnp.sum(p, axis=-1, keepdims=True)
        acc = alpha * acc + _dot(p.astype(BF16), v_ref[pl.ds(start, tq), :])
        return m_new, l_new, acc

    init = (jnp.full((tq, 1), -jnp.inf, F32), jnp.zeros((tq, 1), F32), jnp.zeros((tq, MLA_V), F32))
    carry = block(qi, True, *init)
    m_run, l_run, acc = lax.fori_loop(0, qi, lambda t, c: block(t, False, *c), carry)
    o_ref[...] = (acc / l_run).astype(o_ref.dtype)


def mla_attention(q, tab, kv, krot, *, batch, seq, tq=128):
    nq = seq // tq
    hds = MLA_HEADS
    return pl.pallas_call(
        _mla_kernel,
        grid=(batch, hds, nq),
        in_specs=[
            pl.BlockSpec((tq, 2 * LANE), lambda b, h, i: (b * nq + i, h)),
            pl.BlockSpec((tq, LANE), lambda b, h, i: (b * nq + i, 0)),
            pl.BlockSpec((seq, MLA_NOPE), lambda b, h, i: (b, 2 * h)),
            pl.BlockSpec((seq, LANE), lambda b, h, i: (b, 0)),
            pl.BlockSpec((seq, MLA_V), lambda b, h, i: (b, 2 * h + 1)),
        ],
        out_specs=pl.BlockSpec((tq, MLA_V), lambda b, h, i: (b * nq + i, h)),
        out_shape=jax.ShapeDtypeStruct((batch * seq, hds * MLA_V), BF16),
        compiler_params=_params(("parallel", "parallel", "arbitrary")),
        name="mla_attention",
    )(q, tab, kv, krot, kv)


def _gla_kernel(q_ref, k_ref, v_ref, r_ref, lg_ref, w2_ref, b2_ref, gn_ref, o_ref, state_ref):
    t = pl.program_id(2)
    tt = q_ref.shape[0]
    c = GLA_CHUNK
    grp = LANE

    @pl.when(t == 0)
    def _():
        state_ref[...] = jnp.zeros_like(state_ref)

    x = _dot(lg_ref[...].astype(BF16), w2_ref[...]) + b2_ref[...]
    log_a = -_softplus(-x) * (1.0 / GLA_GATE_TAU)

    ri = lax.broadcasted_iota(jnp.int32, (grp, grp), 0)
    ci = lax.broadcasted_iota(jnp.int32, (grp, grp), 1)
    same = (ri // c) == (ci // c)
    tri = same & (ci <= ri)
    lower = jnp.where(tri, 1.0, 0.0).astype(BF16)
    upper = jnp.where(same & (ci > ri), 1.0, 0.0).astype(BF16)
    scale = GLA_DK ** -0.5
    gn = gn_ref[...]

    state = state_ref[...]
    for gi in range(tt // grp):
        sl = slice(gi * grp, (gi + 1) * grp)
        la = log_a[sl]
        cum = _dot_exact_rhs(lower, la)
        rest = _dot_exact_rhs(upper, la)
        q_t = (q_ref[sl, :] * scale * jnp.exp(cum)).astype(BF16)
        kk = k_ref[sl, :]
        k_t = (kk * jnp.exp(-cum)).astype(BF16)
        k_dec = (kk * jnp.exp(rest)).astype(BF16)
        a_last = jnp.exp(cum + rest)
        v16 = v_ref[sl, :].astype(BF16)
        attn = jnp.where(tri, _dot_nt(q_t, k_t), 0.0)
        o_intra = _dot(attn.astype(BF16), v16)
        inter = []
        for ch in range(grp // c):
            cs = slice(ch * c, (ch + 1) * c)
            inter.append(_dot_nt(q_t[cs], state.astype(BF16)))
            state = state * a_last[ch * c:ch * c + 1, :] + _dot_tn(v16[cs], k_dec[cs])
        o = jnp.concatenate(inter, axis=0) + o_intra
        r = r_ref[sl, :]
        o_ref[sl, :] = (_rms(o, gn) * (r * _sigmoid(r))).astype(o_ref.dtype)
    state_ref[...] = state


def gla_mixer(proj, w2p, b2, gla_norm, *, batch, seq, q_blk, k_blk, v_blk, r_blk, lg_blk, tt=256):
    tt = min(tt, seq)
    nt = seq // tt
    hds = GLA_HEADS
    return pl.pallas_call(
        _gla_kernel,
        grid=(batch, hds, nt),
        in_specs=[
            pl.BlockSpec((tt, GLA_DK), lambda b, h, t: (b * nt + t, q_blk + h)),
            pl.BlockSpec((tt, GLA_DK), lambda b, h, t: (b * nt + t, k_blk + h)),
            pl.BlockSpec((tt, GLA_DV), lambda b, h, t: (b * nt + t, v_blk + h)),
            pl.BlockSpec((tt, GLA_DV), lambda b, h, t: (b * nt + t, r_blk + h)),
            pl.BlockSpec((tt, LANE), lambda b, h, t: (b * nt + t, lg_blk)),
            pl.BlockSpec((LANE, GLA_DK), lambda b, h, t: (0, h)),
            pl.BlockSpec((1, GLA_DK), lambda b, h, t: (0, h)),
            pl.BlockSpec((1, GLA_DV), lambda b, h, t: (0, 0)),
        ],
        out_specs=pl.BlockSpec((tt, GLA_DV), lambda b, h, t: (b * nt + t, h)),
        out_shape=jax.ShapeDtypeStruct((batch * seq, hds * GLA_DV), BF16),
        scratch_shapes=[pltpu.VMEM((GLA_DV, GLA_DK), F32)],
        compiler_params=_params(("parallel", "parallel", "arbitrary")),
        name="gla",
    )(proj, proj, proj, proj, proj, w2p, b2.reshape(1, -1).astype(F32), gla_norm.reshape(1, GLA_DV))


def _even_layer(h, g, w_in, sconv, a_log, dt_bias, gdn_norm, w_out, *, batch, seq):
    sb_w = SB_HEADS * SB_HEAD_DIM
    gw = GDN_HEADS * GDN_D
    w_sb = w_in[:, :3 * sb_w].astype(BF16)
    o0 = 3 * sb_w
    w_qkv = w_in[:, o0:o0 + 3 * gw]
    w_ab = w_in[:, o0 + 3 * gw:o0 + 3 * gw + 2 * GDN_HEADS]
    w_gate = w_in[:, o0 + 3 * gw + 2 * GDN_HEADS:]
    pad = jnp.zeros((w_in.shape[0], LANE - 2 * GDN_HEADS), w_in.dtype)
    w_g = jnp.concatenate([w_qkv, w_gate, w_ab, pad], axis=1).astype(BF16)
    sb_qkv = norm_matmul(h, g, w_sb, BF16)
    gproj = norm_matmul(h, g, w_g, F32, tn=3 * LANE)
    o_a = sb_attention(sb_qkv, batch=batch, seq=seq)
    o_b = gdn_mixer(gproj, 4 * gw // LANE, sconv.astype(F32), a_log, dt_bias, gdn_norm, batch=batch, seq=seq)
    wo = w_out.astype(BF16)
    return out_proj(h, o_a, o_b, wo[:sb_w], wo[sb_w:])


def _odd_layer(h, pos, g, w_in, q_norm, kv_norm, w_uq, w_ukv, gla_w2, gla_b2, gla_norm, w_out, *, batch, seq):
    r = MLA_RANK
    half = MLA_ROPE // 2
    gk = GLA_HEADS * GLA_DK
    gv = GLA_HEADS * GLA_DV
    o = 0
    w_cq = w_in[:, o:o + r]; o += r
    w_ckv = w_in[:, o:o + r]; o += r
    w_kr = w_in[:, o:o + MLA_ROPE]; o += MLA_ROPE
    w_lq = w_in[:, o:o + gk]; o += gk
    w_lk = w_in[:, o:o + gk]; o += gk
    w_lv = w_in[:, o:o + gv]; o += gv
    w_lg = w_in[:, o:o + GLA_GATE_RANK]; o += GLA_GATE_RANK
    w_lr = w_in[:, o:o + gv]
    w_kr_sw = jnp.concatenate([w_kr[:, half:], w_kr[:, :half]], axis=1)
    pad = jnp.zeros((w_in.shape[0], LANE - GLA_GATE_RANK), w_in.dtype)
    w_all = jnp.concatenate([w_cq, w_ckv, w_lq, w_lk, w_lv, w_lr, w_kr, w_kr_sw, w_lg, pad], axis=1).astype(BF16)
    proj = norm_matmul(h, g, w_all, F32, tn=LANE * 2)

    qk = MLA_NOPE + MLA_ROPE
    wq = w_uq.reshape(r, MLA_HEADS, qk)
    wq = jnp.concatenate([wq, wq[:, :, MLA_NOPE + half:], wq[:, :, MLA_NOPE:MLA_NOPE + half]], axis=2)
    wq = wq.reshape(r, MLA_HEADS * 2 * LANE).astype(BF16)
    q = norm_matmul(proj, q_norm, wq, F32, x_col_block=0)
    kv = norm_matmul(proj, kv_norm, w_ukv.astype(BF16), BF16, x_col_block=1)
    tab, krot = rope_prep(pos, proj, 32)
    o_c = mla_attention(q, tab, kv, krot, batch=batch, seq=seq)

    w2p = jnp.concatenate([gla_w2, jnp.zeros((LANE - GLA_GATE_RANK, gk), gla_w2.dtype)], axis=0).astype(BF16)
    o_d = gla_mixer(proj, w2p, gla_b2, gla_norm, batch=batch, seq=seq,
                    q_blk=8, k_blk=12, v_blk=8, r_blk=12, lg_blk=33)
    wo = w_out.astype(BF16)
    n_c = MLA_HEADS * MLA_V
    return out_proj(h, o_c, o_d, wo[:n_c], wo[n_c:])


def kernel(x, mem, positions, norm_mix, norm_xattn, norm_ffn, mem_norm, final_norm, ev_w_in, ev_sconv, ev_a_log, ev_dt_bias, ev_gdn_norm, ev_w_out, od_w_in, od_q_norm, od_kv_norm, od_w_uq, od_w_ukv, od_gla_w2, od_gla_b2, od_gla_norm, od_w_out, xa_wq, xa_wk, xa_wv, xa_wo, ffn_w_in, ffn_conv, ffn_conv_b, ffn_w_out):
    batch, seq, d = x.shape
    mem_len = mem.shape[1]
    depth = norm_mix.shape[0]
    h = x.reshape(batch * seq, d)
    mem2 = mem.reshape(batch * mem_len, d)
    pos = positions.reshape(batch * seq, 1).astype(jnp.int32)
    for layer in range(depth):
        if layer % 2 == 0:
            e = layer // 2
            h = _even_layer(h, norm_mix[layer], ev_w_in[e], ev_sconv[e], ev_a_log[e], ev_dt_bias[e],
                            ev_gdn_norm[e], ev_w_out[e], batch=batch, seq=seq)
        else:
            o = layer // 2
            h = _odd_layer(h, pos, norm_mix[layer], od_w_in[o], od_q_norm[o], od_kv_norm[o], od_w_uq[o],
                           od_w_ukv[o], od_gla_w2[o], od_gla_b2[o], od_gla_norm[o], od_w_out[o],
                           batch=batch, seq=seq)
        w_kv = jnp.concatenate([xa_wk[layer], xa_wv[layer]], axis=1).astype(BF16)
        mem_kv = norm_matmul(mem2, mem_norm, w_kv, BF16, tm=mem_len)
        h = cross_attention(h, norm_xattn[layer], xa_wq[layer].astype(BF16), mem_kv, mem_kv,
                            xa_wo[layer].astype(BF16), seq=seq, mem_len=mem_len)
        h = conv_ffn(h, norm_ffn[layer], ffn_w_in[layer].astype(BF16), ffn_conv[layer], ffn_conv_b[layer],
                     ffn_w_out[layer].astype(BF16), seq=seq)
    return rmsnorm_rows(h, final_norm).reshape(batch, seq, d)
```
